```python
import jax, jax.numpy as jnp
from jax import lax
import numpy as np

D_MODEL = 1024
BATCH = 8
SEQ = 2048
DEPTH = 4

CONV_WIDTH = D_MODEL // 2
LRU_WIDTH = D_MODEL // 2
LRU_HEADS = 8
LRU_HEAD_DIM = LRU_WIDTH // LRU_HEADS
LRU_C = 8.0
SHORT_CONV_W = 3
REC_CONV_W = 4
IN_AB = 3 * CONV_WIDTH + 2 * LRU_WIDTH
MIX_AB = CONV_WIDTH + LRU_WIDTH
N_HEADS = 16
HEAD_DIM = D_MODEL // N_HEADS
ATTN_WIDTH = N_HEADS * HEAD_DIM
ROT_DIM = HEAD_DIM // 4
ROPE_THETA = 500000.0
DILATED_PATTERNS = ((128, 1), (512, 4), (2048, 16))
QUERY_BLOCK = 64
N_EXPERTS = 16
N_GROUPS = 4
EXPERTS_PER_GROUP = N_EXPERTS // N_GROUPS
TOP_K = 2
EXPERT_FF = D_MODEL // 2
DISPATCH_BLOCK = 128
N_EVEN = (DEPTH + 1) // 2
N_ODD = DEPTH // 2
EPS = 1e-6
NEG_INF = -1e30

kernel_name = "hybrid_conv_lru_dilated_attn_grouped_moe_encoder"


def rmsnorm(x, g):
    xf = x.astype(jnp.float32)
    y = xf * lax.rsqrt(jnp.mean(xf * xf, axis=-1, keepdims=True) + EPS)
    return (y * g.astype(jnp.float32)).astype(x.dtype)


def modulate(h, shift, scale):
    return h * (1 + scale[:, None, :]) + shift[:, None, :]


def depthwise_conv(u, w, left):
    K = w.shape[0]
    S = u.shape[1]
    up = jnp.pad(u, ((0, 0), (left, K - 1 - left), (0, 0)))
    out = up[:, 0:S] * w[0]
    for k in range(1, K):
        out = out + up[:, k:k + S] * w[k]
    return out


def _linear_recurrence_combine(e1, e2):
    a1, b1 = e1
    a2, b2 = e2
    return a1 * a2, a2 * b1 + b2


def rglru(xr, wa, ba, wx, bx, lam, reverse):
    Bz, S, C = xr.shape
    xh = xr.reshape(Bz, S, LRU_HEADS, LRU_HEAD_DIM)
    r = jax.nn.sigmoid(jnp.einsum('bshi,hij->bshj', xh, wa).reshape(Bz, S, C) + ba).astype(jnp.float32)
    i = jax.nn.sigmoid(jnp.einsum('bshi,hij->bshj', xh, wx).reshape(Bz, S, C) + bx).astype(jnp.float32)
    log_a = -LRU_C * r * jax.nn.softplus(-lam.astype(jnp.float32))
    a = jnp.exp(log_a)
    mult = jnp.sqrt(-jnp.expm1(2.0 * log_a))
    b = xr.astype(jnp.float32) * i * mult
    _, h = lax.associative_scan(_linear_recurrence_combine, (a, b), axis=1, reverse=reverse)
    return h


def conv_lru_mixer(h, w_in, conv_a_w, conv_b_w, conv_b_b, lru_wa, lru_ba, lru_wx, lru_bx, lru_lambda, w_out):
    z = h @ w_in
    cw, lw = CONV_WIDTH, LRU_WIDTH
    a_val = z[..., 0:cw]
    a_gb = z[..., cw:2 * cw]
    a_gc = z[..., 2 * cw:3 * cw]
    b_gate = z[..., 3 * cw:3 * cw + lw]
    b_in = z[..., 3 * cw + lw:3 * cw + 2 * lw]
    y_a = a_gb * depthwise_conv(a_gc * a_val, conv_a_w, 1)
    xr = depthwise_conv(b_in, conv_b_w, 1) + conv_b_b
    h_rec = (rglru(xr, lru_wa[0], lru_ba[0], lru_wx[0], lru_bx[0], lru_lambda[0], False)
             + rglru(xr, lru_wa[1], lru_ba[1], lru_wx[1], lru_bx[1], lru_lambda[1], True))
    y_b = jax.nn.gelu(b_gate) * h_rec.astype(h.dtype)
    return jnp.concatenate([y_a, y_b], axis=-1) @ w_out


def partial_rotary(t, positions):
    inv_freq = ROPE_THETA ** (-jnp.arange(0, ROT_DIM, 2, dtype=jnp.float32) / ROT_DIM)
    ang = positions.astype(jnp.float32)[..., None] * inv_freq
    cos = jnp.cos(ang)[:, :, None, :]
    sin = jnp.sin(ang)[:, :, None, :]
    tr = t[..., :ROT_DIM].astype(jnp.float32)
    t1, t2 = tr[..., :ROT_DIM // 2], tr[..., ROT_DIM // 2:]
    rot = jnp.concatenate([t1 * cos - t2 * sin, t2 * cos + t1 * sin], axis=-1)
    return jnp.concatenate([rot.astype(t.dtype), t[..., ROT_DIM:]], axis=-1)


def band_attention(q, k, v, radius):
    Bz, G, L, H, hd = q.shape
    qb = QUERY_BLOCK
    nb = -(-L // qb)
    Lp = nb * qb
    span = qb + 2 * radius
    qp = jnp.pad(q, ((0, 0), (0, 0), (0, Lp - L), (0, 0), (0, 0)))
    kp = jnp.pad(k, ((0, 0), (0, 0), (radius, radius + Lp - L), (0, 0), (0, 0)))
    vp = jnp.pad(v, ((0, 0), (0, 0), (radius, radius + Lp - L), (0, 0), (0, 0)))
    kidx = jnp.arange(nb)[:, None] * qb + jnp.arange(span)[None, :]
    kb = kp[:, :, kidx]
    vb = vp[:, :, kidx]
    qblk = qp.reshape(Bz, G, nb, qb, H, hd)
    qpos = jnp.arange(nb)[:, None] * qb + jnp.arange(qb)[None, :]
    kpos = kidx - radius
    valid = ((jnp.abs(qpos[:, :, None] - kpos[:, None, :]) <= radius)
             & (kpos[:, None, :] >= 0) & (kpos[:, None, :] < L))
    s = jnp.einsum('bgnqhd,bgnkhd->bgnhqk', qblk, kb).astype(jnp.float32) * (hd ** -0.5)
    s = jnp.where(valid[:, None], s, NEG_INF)
    m = jnp.max(s, axis=-1, keepdims=True)
    p = jnp.exp(s - m)
    den = jnp.sum(p, axis=-1, keepdims=True)
    o = jnp.einsum('bgnhqk,bgnkhd->bgnqhd', p, vb.astype(jnp.float32)) / jnp.swapaxes(den, 3, 4)
    lse = jnp.swapaxes((m + jnp.log(den))[..., 0], 3, 4)
    o = o.reshape(Bz, G, Lp, H, hd)[:, :, :L]
    lse = lse.reshape(Bz, G, Lp, H)[:, :, :L]
    return o, lse


def dilated_attention(q, k, v):
    Bz, S, H, hd = q.shape
    outs, lses = [], []
    for (w, d) in DILATED_PATTERNS:
        radius = w // (2 * d)
        def to_sub(t):
            return t.reshape(Bz, S // d, d, H, hd).transpose(0, 2, 1, 3, 4)
        o, lse = band_attention(to_sub(q), to_sub(k), to_sub(v), radius)
        outs.append(o.transpose(0, 2, 1, 3, 4).reshape(Bz, S, H, hd))
        lses.append(lse.transpose(0, 2, 1, 3).reshape(Bz, S, H))
    wts = jax.nn.softmax(jnp.stack(lses, axis=-1), axis=-1)
    return jnp.einsum('bshp,pbshd->bshd', wts, jnp.stack(outs, axis=0))


def dilated_attn_mixer(h, positions, w_qkv, w_o):
    Bz, S, _ = h.shape
    qkv = (h @ w_qkv).reshape(Bz, S, 3, N_HEADS, HEAD_DIM)
    q = partial_rotary(qkv[:, :, 0], positions)
    k = partial_rotary(qkv[:, :, 1], positions)
    v = qkv[:, :, 2]
    o = dilated_attention(q, k, v).astype(h.dtype)
    return o.reshape(Bz, S, ATTN_WIDTH) @ w_o


def grouped_moe(h, router_w, router_bias, w_gate, w_up, w_down):
    Bz, S, D = h.shape
    T = Bz * S
    hf = h.reshape(T, D)
    scores = jax.nn.sigmoid((hf @ router_w).astype(jnp.float32))
    biased = scores + router_bias.astype(jnp.float32)
    grouped = biased.reshape(T, N_GROUPS, EXPERTS_PER_GROUP)
    group_score = jnp.sum(lax.top_k(grouped, 2)[0], axis=-1)
    best_group = jnp.argmax(group_score, axis=-1)
    in_group = (jnp.arange(N_EXPERTS) // EXPERTS_PER_GROUP)[None, :] == best_group[:, None]
    _, top_idx = lax.top_k(jnp.where(in_group, biased, -jnp.inf), TOP_K)
    top_s = jnp.take_along_axis(scores, top_idx, axis=-1)
    top_w = top_s / jnp.sum(top_s, axis=-1, keepdims=True)
    n_assign = T * TOP_K
    e_flat = top_idx.reshape(-1)
    w_flat = top_w.reshape(-1)
    tok_flat = jnp.arange(n_assign, dtype=jnp.int32) // TOP_K
    order = jnp.argsort(e_flat)
    e_sorted = e_flat[order]
    counts = jnp.zeros((N_EXPERTS,), jnp.int32).at[e_flat].add(1)
    padded = ((counts + DISPATCH_BLOCK - 1) // DISPATCH_BLOCK) * DISPATCH_BLOCK
    start = jnp.cumsum(counts) - counts
    pend = jnp.cumsum(padded)
    pstart = pend - padded
    dest = pstart[e_sorted] + (jnp.arange(n_assign, dtype=jnp.int32) - start[e_sorted])
    cap = n_assign + N_EXPERTS * DISPATCH_BLOCK
    n_blocks = cap // DISPATCH_BLOCK
    slot_tok = jnp.full((cap,), T, jnp.int32).at[dest].set(tok_flat[order])
    slot_w = jnp.zeros((cap,), jnp.float32).at[dest].set(w_flat[order])
    block_start = jnp.arange(n_blocks, dtype=jnp.int32) * DISPATCH_BLOCK
    block_expert = jnp.minimum(jnp.searchsorted(pend, block_start, side='right'), N_EXPERTS - 1)
    hpad = jnp.concatenate([hf, jnp.zeros((1, D), hf.dtype)], axis=0)
    xs = hpad[slot_tok].reshape(n_blocks, DISPATCH_BLOCK, D)

    def expert_block(args):
        xb, e = args
        return (jax.nn.silu(xb @ w_gate[e]) * (xb @ w_up[e])) @ w_down[e]

    yb = lax.map(expert_block, (xs, block_expert)).reshape(cap, D)
    out = jnp.zeros((T + 1, D), jnp.float32).at[slot_tok].add(yb.astype(jnp.float32) * slot_w[:, None])
    return out[:T].reshape(Bz, S, D).astype(h.dtype)


def setup_inputs(seed: int = 0) -> dict:
    key = jax.random.key(seed)
    ks = iter(jax.random.split(key, 32))
    nrm = lambda shape, scale: jax.random.normal(next(ks), shape, jnp.float32) * scale
    x = nrm((BATCH, SEQ, D_MODEL), 1.0)
    c = nrm((BATCH, D_MODEL), 1.0)
    offset = jax.random.randint(next(ks), (BATCH, 1), 0, 1024, dtype=jnp.int32)
    positions = (jnp.arange(SEQ, dtype=jnp.int32)[None, :] + offset).astype(jnp.int32)
    mod_w = nrm((DEPTH, D_MODEL, 6 * D_MODEL), 0.5 * D_MODEL ** -0.5)
    mod_b = nrm((DEPTH, 6 * D_MODEL), 0.02)
    norm_mix_g = 1.0 + nrm((DEPTH, D_MODEL), 0.02)
    norm_ffn_g = 1.0 + nrm((DEPTH, D_MODEL), 0.02)
    final_norm_g = 1.0 + nrm((D_MODEL,), 0.02)
    w_in_ab = nrm((N_EVEN, D_MODEL, IN_AB), D_MODEL ** -0.5)
    conv_a_w = nrm((N_EVEN, SHORT_CONV_W, CONV_WIDTH), SHORT_CONV_W ** -0.5)
    conv_b_w = nrm((N_EVEN, REC_CONV_W, LRU_WIDTH), REC_CONV_W ** -0.5)
    conv_b_b = nrm((N_EVEN, LRU_WIDTH), 0.02)
    lru_wa = nrm((N_EVEN, 2, LRU_HEADS, LRU_HEAD_DIM, LRU_HEAD_DIM), LRU_HEAD_DIM ** -0.5)
    lru_ba = nrm((N_EVEN, 2, LRU_WIDTH), 0.02)
    lru_wx = nrm((N_EVEN, 2, LRU_HEADS, LRU_HEAD_DIM, LRU_HEAD_DIM), LRU_HEAD_DIM ** -0.5)
    lru_bx = nrm((N_EVEN, 2, LRU_WIDTH), 0.02)
    u = jax.random.uniform(next(ks), (N_EVEN, 2, LRU_WIDTH), jnp.float32, 0.9, 0.999)
    s_lam = u ** (1.0 / LRU_C)
    lru_lambda = jnp.log(s_lam) - jnp.log1p(-s_lam)
    w_out_ab = nrm((N_EVEN, MIX_AB, D_MODEL), MIX_AB ** -0.5)
    w_qkv = nrm((N_ODD, D_MODEL, 3 * ATTN_WIDTH), D_MODEL ** -0.5)
    w_o = nrm((N_ODD, ATTN_WIDTH, D_MODEL), ATTN_WIDTH ** -0.5)
    router_w = nrm((D_MODEL, N_EXPERTS), D_MODEL ** -0.5)
    router_bias = nrm((N_EXPERTS,), 0.01)
    expert_w_gate = nrm((DEPTH, N_EXPERTS, D_MODEL, EXPERT_FF), D_MODEL ** -0.5)
    expert_w_up = nrm((DEPTH, N_EXPERTS, D_MODEL, EXPERT_FF), D_MODEL ** -0.5)
    expert_w_down = nrm((DEPTH, N_EXPERTS, EXPERT_FF, D_MODEL), EXPERT_FF ** -0.5)
    return {"x": x, "c": c, "positions": positions, "mod_w": mod_w, "mod_b": mod_b,
            "norm_mix_g": norm_mix_g, "norm_ffn_g": norm_ffn_g, "final_norm_g": final_norm_g,
            "w_in_ab": w_in_ab, "conv_a_w": conv_a_w, "conv_b_w": conv_b_w, "conv_b_b": conv_b_b,
            "lru_wa": lru_wa, "lru_ba": lru_ba, "lru_wx": lru_wx, "lru_bx": lru_bx,
            "lru_lambda": lru_lambda, "w_out_ab": w_out_ab, "w_qkv": w_qkv, "w_o": w_o,
            "router_w": router_w, "router_bias": router_bias, "expert_w_gate": expert_w_gate,
            "expert_w_up": expert_w_up, "expert_w_down": expert_w_down}


def reference(x, c, positions, mod_w, mod_b, norm_mix_g, norm_ffn_g, final_norm_g,
              w_in_ab, conv_a_w, conv_b_w, conv_b_b, lru_wa, lru_ba, lru_wx, lru_bx,
              lru_lambda, w_out_ab, w_qkv, w_o, router_w, router_bias,
              expert_w_gate, expert_w_up, expert_w_down):
    cs = jax.nn.silu(c)
    for l in range(DEPTH):
        mod = cs @ mod_w[l] + mod_b[l]
        sh1, sc1, g1, sh2, sc2, g2 = jnp.split(mod, 6, axis=-1)
        h = modulate(rmsnorm(x, norm_mix_g[l]), sh1, sc1)
        if l % 2 == 0:
            j = l // 2
            y = conv_lru_mixer(h, w_in_ab[j], conv_a_w[j], conv_b_w[j], conv_b_b[j], lru_wa[j], lru_ba[j],
                               lru_wx[j], lru_bx[j], lru_lambda[j], w_out_ab[j])
        else:
            j = l // 2
            y = dilated_attn_mixer(h, positions, w_qkv[j], w_o[j])
        x = x + g1[:, None, :] * y
        h = modulate(rmsnorm(x, norm_ffn_g[l]), sh2, sc2)
        x = x + g2[:, None, :] * grouped_moe(h, router_w, router_bias,
                                             expert_w_gate[l], expert_w_up[l], expert_w_down[l])
    return rmsnorm(x, final_norm_g)
```

```python
import functools

import jax
import jax.numpy as jnp
from jax import lax
from jax.experimental import pallas as pl
from jax.experimental.pallas import tpu as pltpu

F32 = jnp.float32
BF16 = jnp.bfloat16
HIGHEST = lax.Precision.HIGHEST

LRU_HEADS = 8
LRU_C = 8.0
N_HEADS = 16
HEAD_DIM = 64
ROT_DIM = HEAD_DIM // 4
ROPE_THETA = 500000.0
ATTN_RADIUS = 64
N_EXPERTS = 16
N_GROUPS = 4
EXPERTS_PER_GROUP = 4
PAIRS_PER_GROUP = 6
N_CLASSES = N_GROUPS * PAIRS_PER_GROUP
EPS = 1e-6
NEG_INF = -1e30

LANES = 128
SUBLANES = 8
MIB = 1024 * 1024

TOKEN_TILE = 512
ROUTER_TILE = 256
EXPERT_ROWS = 128
EXT_COLS = LANES
DMA_WINDOW = 32
PERMUTE_CHUNK = 2048


def _params(n_axes, vmem_mib):
    return pltpu.CompilerParams(dimension_semantics=("arbitrary",) * n_axes,
                                vmem_limit_bytes=vmem_mib * MIB)


def _prenorm(x, g, scale, shift):
    ms = jnp.mean(x * x, axis=-1, keepdims=True)
    return (x * lax.rsqrt(ms + EPS) * g) * (1.0 + scale) + shift


def _mod_body(c_ref, w_ref, b_ref, o_ref):
    cs = jax.nn.silu(c_ref[...])
    o_ref[0] = jnp.dot(cs, w_ref[0], precision=HIGHEST, preferred_element_type=F32) + b_ref[0]


def _modulation(c, mod_w, mod_b):
    depth, d, n = mod_w.shape
    bsz = c.shape[0]
    tn = 1536
    return pl.pallas_call(
        _mod_body,
        grid=(depth, n // tn),
        in_specs=[pl.BlockSpec((bsz, d), lambda l, j: (0, 0)),
                  pl.BlockSpec((1, d, tn), lambda l, j: (l, 0, j)),
                  pl.BlockSpec((1, 1, tn), lambda l, j: (l, 0, j))],
        out_specs=pl.BlockSpec((1, bsz, tn), lambda l, j: (l, 0, j)),
        out_shape=jax.ShapeDtypeStruct((depth, bsz, n), F32),
        compiler_params=_params(2, 40),
        name="modulation",
    )(c, mod_w, mod_b.reshape(depth, 1, n))


def _inproj_body(*refs, has_delta, rope, d_model, tn):
    refs = list(refs)
    x_ref = refs.pop(0)
    if has_delta:
        delta_ref = refs.pop(0)
        modp_ref = refs.pop(0)
    mod_ref = refs.pop(0)
    g_ref = refs.pop(0)
    w_ref = refs.pop(0)
    if rope:
        pos_ref = refs.pop(0)
        invf_ref = refs.pop(0)
    if has_delta:
        xn_ref = refs.pop(0)
    z_ref = refs.pop(0)

    x = x_ref[...]
    if has_delta:
        x = x + modp_ref[0][5:6] * delta_ref[...]
        xn_ref[...] = x
    mod = mod_ref[0]
    h = _prenorm(x, g_ref[...], mod[1:2], mod[0:1]).astype(BF16)

    if rope:
        lane = lax.broadcasted_iota(jnp.int32, (1, LANES), 1)
        p = lane % HEAD_DIM
        ang = pos_ref[...].astype(F32) * invf_ref[...]
        cos = jnp.cos(ang)
        sin = jnp.sin(ang)
        c_tab = jnp.where(p < ROT_DIM, cos, 1.0)
        s_up = jnp.where((p >= ROT_DIM // 2) & (p < ROT_DIM), sin, 0.0)
        s_dn = jnp.where(p < ROT_DIM // 2, -sin, 0.0)
        q_scale = HEAD_DIM ** -0.5

    n = w_ref.shape[1]
    for j in range(n // tn):
        acc = jnp.dot(h, w_ref[:, j * tn:(j + 1) * tn], preferred_element_type=F32)
        if rope and j * tn < 2 * d_model:
            sc = q_scale if j * tn < d_model else 1.0
            parts = []
            for u in range(tn // LANES):
                t = acc[:, u * LANES:(u + 1) * LANES]
                rot = (t * c_tab + pltpu.roll(t, ROT_DIM // 2, 1) * s_up
                       + pltpu.roll(t, LANES - ROT_DIM // 2, 1) * s_dn)
                parts.append(rot * sc)
            acc = jnp.concatenate(parts, axis=1)
        z_ref[:, j * tn:(j + 1) * tn] = acc.astype(z_ref.dtype)


def _inproj(x, delta, mod_prev, mod, g, w, seq, positions=None, inv_freq=None):
    t, d = x.shape
    n = w.shape[1]
    tm = TOKEN_TILE
    per_seq = seq // tm
    has_delta = delta is not None
    rope = positions is not None
    row = pl.BlockSpec((tm, d), lambda i: (i, 0))
    modspec = pl.BlockSpec((1, 6, d), lambda i: (i // per_seq, 0, 0))
    args, specs = [x], [row]
    if has_delta:
        args += [delta, mod_prev]
        specs += [row, modspec]
    args += [mod, g, w]
    specs += [modspec, pl.BlockSpec((1, d), lambda i: (0, 0)), pl.BlockSpec((d, n), lambda i: (0, 0))]
    if rope:
        args += [positions, inv_freq]
        specs += [pl.BlockSpec((tm, 1), lambda i: (i, 0)), pl.BlockSpec((1, LANES), lambda i: (0, 0))]
    out_shape, out_specs = [], []
    if has_delta:
        out_shape.append(jax.ShapeDtypeStruct((t, d), F32))
        out_specs.append(row)
    out_shape.append(jax.ShapeDtypeStruct((t, n), BF16))
    out_specs.append(pl.BlockSpec((tm, n), lambda i: (i, 0)))
    outs = pl.pallas_call(
        functools.partial(_inproj_body, has_delta=has_delta, rope=rope, d_model=d, tn=512),
        grid=(t // tm,),
        in_specs=specs,
        out_specs=out_specs,
        out_shape=out_shape,
        compiler_params=_params(1, 56),
        name="inproj_rope" if rope else "inproj",
    )(*args)
    if has_delta:
        return outs[0], outs[1]
    return x, outs[0]


def _gelu_tanh(x):
    return 0.5 * x * (1.0 + jnp.tanh(0.7978845608028654 * (x + 0.044715 * x * x * x)))


def _mixer_body(val_ref, gb_ref, gc_ref, gate_ref, bin_ref, cwa_ref, cwb_ref, cbb_ref, wg_ref,
                ba_ref, bx_ref, lam_ref, ya_ref, yb_ref, af_ref, bf_ref, ab_ref, bb_ref, hf_ref, hb_ref):
    s = val_ref.shape[0]
    row = lax.broadcasted_iota(jnp.int32, (s, LANES), 0)
    r8 = row & (SUBLANES - 1)

    def shifted(u, k):
        if k == 0:
            return u
        rolled = pltpu.roll(u, (-k) % s, 0)
        ok = (row + k >= 0) & (row + k < s)
        return jnp.where(ok, rolled, 0.0)

    u = gc_ref[...].astype(F32) * val_ref[...].astype(F32)
    cwa = cwa_ref[...]
    conv_a = cwa[0:1] * shifted(u, -1) + cwa[1:2] * u + cwa[2:3] * shifted(u, 1)
    ya_ref[...] = (gb_ref[...].astype(F32) * conv_a).astype(ya_ref.dtype)

    b_in = bin_ref[...].astype(F32)
    cwb = cwb_ref[...]
    xr = (cwb[0:1] * shifted(b_in, -1) + cwb[1:2] * b_in + cwb[2:3] * shifted(b_in, 1)
          + cwb[3:4] * shifted(b_in, 2)) + cbb_ref[...]

    pre = jnp.dot(xr.astype(BF16), wg_ref[0], preferred_element_type=F32)
    lam = lam_ref[...]
    neg = -lam
    softplus = jnp.maximum(neg, 0.0) + jnp.log1p(jnp.exp(-jnp.abs(neg)))

    def scan_inputs(direction):
        r = jax.nn.sigmoid(pre[:, (2 * direction) * LANES:(2 * direction + 1) * LANES]
                           + ba_ref[direction:direction + 1, :])
        i = jax.nn.sigmoid(pre[:, (2 * direction + 1) * LANES:(2 * direction + 2) * LANES]
                           + bx_ref[direction:direction + 1, :])
        log_a = -LRU_C * r * softplus[direction:direction + 1, :]
        a = jnp.exp(log_a)
        mult = jnp.sqrt(1.0 - a * a)
        return a, xr * i * mult

    def scan_in_groups(a, b, reverse):
        for dist in (1, 2, 4):
            if reverse:
                ok = r8 < SUBLANES - dist
                sh = s - dist
            else:
                ok = r8 >= dist
                sh = dist
            a_n = jnp.where(ok, pltpu.roll(a, sh, 0), 1.0)
            b_n = jnp.where(ok, pltpu.roll(b, sh, 0), 0.0)
            b = a * b_n + b
            a = a * a_n
        return a, b

    a, b = scan_inputs(0)
    a, b = scan_in_groups(a, b, False)
    af_ref[...] = a
    bf_ref[...] = b
    a, b = scan_inputs(1)
    a, b = scan_in_groups(a, b, True)
    ab_ref[...] = a
    bb_ref[...] = b

    n_groups = s // SUBLANES

    def carry_step(i, carry):
        cf, cb = carry
        gf = pl.multiple_of(i * SUBLANES, SUBLANES)
        hf = bf_ref[pl.ds(gf, SUBLANES), :] + af_ref[pl.ds(gf, SUBLANES), :] * cf
        hf_ref[pl.ds(gf, SUBLANES), :] = hf
        gr = pl.multiple_of((n_groups - 1 - i) * SUBLANES, SUBLANES)
        hb = bb_ref[pl.ds(gr, SUBLANES), :] + ab_ref[pl.ds(gr, SUBLANES), :] * cb
        hb_ref[pl.ds(gr, SUBLANES), :] = hb
        return hf[SUBLANES - 1:SUBLANES, :], hb[0:1, :]

    zero = jnp.zeros((1, LANES), F32)
    lax.fori_loop(0, n_groups, carry_step, (zero, zero), unroll=8)

    h_rec = hf_ref[...] + hb_ref[...]
    yb_ref[...] = (_gelu_tanh(gate_ref[...].astype(F32)) * h_rec).astype(yb_ref.dtype)


def _mixer(z, seq, conv_a_w, conv_b_w, conv_b_b, gate_w, lru_ba, lru_bx, lru_lambda):
    t = z.shape[0]
    width = conv_a_w.shape[1]
    nslab = width // LANES
    bsz = t // seq

    def zcol(k):
        return pl.BlockSpec((seq, LANES), lambda b, c, k=k: (b, k * nslab + c))

    def par(rows):
        return pl.BlockSpec((rows, LANES), lambda b, c: (0, c))

    out = pl.BlockSpec((seq, LANES), lambda b, c: (b, c))
    ya, yb = pl.pallas_call(
        _mixer_body,
        grid=(bsz, nslab),
        in_specs=[zcol(0), zcol(1), zcol(2), zcol(3), zcol(4),
                  par(conv_a_w.shape[0]), par(conv_b_w.shape[0]), par(1),
                  pl.BlockSpec((1, LANES, 4 * LANES), lambda b, c: (c, 0, 0)),
                  par(2), par(2), par(2)],
        out_specs=[out, out],
        out_shape=[jax.ShapeDtypeStruct((t, width), BF16)] * 2,
        scratch_shapes=[pltpu.VMEM((seq, LANES), F32)] * 6,
        compiler_params=_params(2, 48),
        name="conv_lru_mixer",
    )(z, z, z, z, z, conv_a_w, conv_b_w, conv_b_b.reshape(1, width), gate_w, lru_ba, lru_bx, lru_lambda)
    return ya, yb


def _gate_weights(lru_wa, lru_wx):
    _, heads, hd, _ = lru_wa.shape
    per = LANES // hd
    nslab = heads // per

    def blockdiag(w):
        w = w.reshape(nslab, per, hd, hd)
        eye = jnp.eye(per, dtype=w.dtype)
        return jnp.einsum('cpij,pq->cpiqj', w, eye).reshape(nslab, LANES, LANES)

    return jnp.concatenate([blockdiag(lru_wa[0]), blockdiag(lru_wx[0]),
                            blockdiag(lru_wa[1]), blockdiag(lru_wx[1])], axis=-1).astype(BF16)


def _attn_body(q_ref, k_ref, v_ref, o_ref, qf, kf, vf, qp, kp, vp, acc, mx, den):
    s = q_ref.shape[0]
    n16 = s // 16
    lane = lax.broadcasted_iota(jnp.int32, (1, LANES), 1)
    head0 = lane < HEAD_DIM

    def attend(qb, kb, vb, valid):
        res = []
        for h in range(2):
            mine = head0 if h == 0 else jnp.logical_not(head0)
            qh = jnp.where(mine, qb, jnp.zeros_like(qb))
            sc = lax.dot_general(qh, kb, (((1,), (1,)), ((), ())), preferred_element_type=F32)
            sc = jnp.where(valid, sc, NEG_INF)
            m = jnp.max(sc, axis=1, keepdims=True)
            p = jnp.exp(sc - m)
            l = jnp.sum(p, axis=1, keepdims=True)
            a = jnp.dot(p.astype(BF16), vb, preferred_element_type=F32)
            res.append((m, l, a))
        m = jnp.where(head0, res[0][0], res[1][0])
        l = jnp.where(head0, res[0][1], res[1][1])
        a = jnp.where(head0, res[0][2], res[1][2])
        return m, l, a

    nq, nk = 128, 256
    iq = lax.broadcasted_iota(jnp.int32, (nq, nk), 0)
    ik = lax.broadcasted_iota(jnp.int32, (nq, nk), 1)

    def d1_step(j, _):
        q0 = pl.multiple_of(j * nq, nq)
        w0 = pl.multiple_of(jnp.clip(j * nq - ATTN_RADIUS, 0, s - nk), ATTN_RADIUS)
        valid = jnp.abs((q0 + iq) - (w0 + ik)) <= ATTN_RADIUS
        m, l, a = attend(q_ref[pl.ds(q0, nq), :], k_ref[pl.ds(w0, nk), :], v_ref[pl.ds(w0, nk), :], valid)
        mx[0, pl.ds(q0, nq), :] = m
        den[0, pl.ds(q0, nq), :] = l
        acc[0, pl.ds(q0, nq), :] = a
        return 0

    lax.fori_loop(0, s // nq, d1_step, 0)

    qf[...] = q_ref[...].astype(F32)
    kf[...] = k_ref[...].astype(F32)
    vf[...] = v_ref[...].astype(F32)
    for r in range(16):
        qp[r * n16:(r + 1) * n16, :] = qf[pl.ds(r, n16, stride=16), :].astype(BF16)
        kp[r * n16:(r + 1) * n16, :] = kf[pl.ds(r, n16, stride=16), :].astype(BF16)
        vp[r * n16:(r + 1) * n16, :] = vf[pl.ds(r, n16, stride=16), :].astype(BF16)

    i16 = lax.broadcasted_iota(jnp.int32, (n16, n16), 0)
    k16 = lax.broadcasted_iota(jnp.int32, (n16, n16), 1)
    valid16 = jnp.abs(i16 - k16) <= ATTN_RADIUS

    def d16_step(r, _):
        base = pl.multiple_of(r * n16, n16)
        m, l, a = attend(qp[pl.ds(base, n16), :], kp[pl.ds(base, n16), :], vp[pl.ds(base, n16), :], valid16)
        mx[2, pl.ds(r, n16, stride=16), :] = m
        den[2, pl.ds(r, n16, stride=16), :] = l
        acc[2, pl.ds(r, n16, stride=16), :] = a
        return 0

    lax.fori_loop(0, 16, d16_step, 0)

    qn, kn = 32, 64
    iq4 = lax.broadcasted_iota(jnp.int32, (4 * qn, 4 * kn), 0)
    ik4 = lax.broadcasted_iota(jnp.int32, (4 * qn, 4 * kn), 1)
    pos_q = 4 * (iq4 % qn) + iq4 // qn
    pos_k = 4 * (ik4 % kn) + ik4 // kn

    def d4_step(r4, _):
        for a0 in range(0, n16, qn):
            w0 = min(max(a0 - 16, 0), n16 - kn)
            valid = jnp.abs((pos_q + 4 * a0) - (pos_k + 4 * w0)) <= ATTN_RADIUS
            qb = jnp.concatenate(
                [qp[pl.ds(pl.multiple_of((r4 + 4 * mm) * n16 + a0, qn), qn), :] for mm in range(4)], axis=0)
            kb = jnp.concatenate(
                [kp[pl.ds(pl.multiple_of((r4 + 4 * mm) * n16 + w0, 16), kn), :] for mm in range(4)], axis=0)
            vb = jnp.concatenate(
                [vp[pl.ds(pl.multiple_of((r4 + 4 * mm) * n16 + w0, 16), kn), :] for mm in range(4)], axis=0)
            m, l, a = attend(qb, kb, vb, valid)
            for mm in range(4):
                dst = pl.ds(16 * a0 + r4 + 4 * mm, qn, stride=16)
                mx[1, dst, :] = m[mm * qn:(mm + 1) * qn]
                den[1, dst, :] = l[mm * qn:(mm + 1) * qn]
                acc[1, dst, :] = a[mm * qn:(mm + 1) * qn]
        return 0

    lax.fori_loop(0, 4, d4_step, 0)

    chunk = 256

    def merge(c, _):
        rows = pl.ds(pl.multiple_of(c * chunk, chunk), chunk)
        m0, m1, m2 = mx[0, rows, :], mx[1, rows, :], mx[2, rows, :]
        top = jnp.maximum(jnp.maximum(m0, m1), m2)
        e0, e1, e2 = jnp.exp(m0 - top), jnp.exp(m1 - top), jnp.exp(m2 - top)
        num = e0 * acc[0, rows, :] + e1 * acc[1, rows, :] + e2 * acc[2, rows, :]
        dn = e0 * den[0, rows, :] + e1 * den[1, rows, :] + e2 * den[2, rows, :]
        o_ref[rows, :] = (num / dn).astype(o_ref.dtype)
        return 0

    lax.fori_loop(0, s // chunk, merge, 0)


def _attention(qkv, seq, width):
    t = qkv.shape[0]
    bsz = t // seq
    npair = width // LANES

    def col(k):
        return pl.BlockSpec((seq, LANES), lambda b, p, k=k: (b, k * npair + p))

    return pl.pallas_call(
        _attn_body,
        grid=(bsz, npair),
        in_specs=[col(0), col(1), col(2)],
        out_specs=pl.BlockSpec((seq, LANES), lambda b, p: (b, p)),
        out_shape=jax.ShapeDtypeStruct((t, width), BF16),
        scratch_shapes=[pltpu.VMEM((seq, LANES), F32)] * 3 + [pltpu.VMEM((seq, LANES), BF16)] * 3
                       + [pltpu.VMEM((3, seq, LANES), F32)] * 3,
        compiler_params=_params(2, 48),
        name="dilated_attention",
    )(qkv, qkv, qkv)


def _outproj_router_body(*refs, n_y, d_model):
    refs = list(refs)
    y_refs = [refs.pop(0) for _ in range(n_y)]
    w_refs = [refs.pop(0) for _ in range(n_y)]
    x_ref, mod_ref, g_ref, rw_ref, rb_ref = [refs.pop(0) for _ in range(5)]
    x2_ref, hext_ref, info_ref, cnt_ref, carry_ref = refs
    tm = x_ref.shape[0]
    step = pl.program_id(0)

    @pl.when(step == 0)
    def _():
        carry_ref[...] = jnp.zeros_like(carry_ref)

    proj = jnp.dot(y_refs[0][...], w_refs[0][...], preferred_element_type=F32)
    for k in range(1, n_y):
        proj = proj + jnp.dot(y_refs[k][...], w_refs[k][...], preferred_element_type=F32)
    mod = mod_ref[0]
    x2 = x_ref[...] + mod[2:3] * proj
    x2_ref[...] = x2
    h2 = _prenorm(x2, g_ref[...], mod[4:5], mod[3:4])
    hext_ref[:, :d_model] = h2

    logits = lax.dot_general(rw_ref[...], h2, (((1,), (1,)), ((), ())),
                             precision=HIGHEST, preferred_element_type=F32)
    scores = jax.nn.sigmoid(logits)
    biased = scores + rb_ref[...]
    b_rows = [biased[e:e + 1, :] for e in range(N_EXPERTS)]
    s_rows = [scores[e:e + 1, :] for e in range(N_EXPERTS)]

    group_scores = []
    for g in range(N_GROUPS):
        v = b_rows[g * EXPERTS_PER_GROUP:(g + 1) * EXPERTS_PER_GROUP]
        best = None
        for i1 in range(EXPERTS_PER_GROUP):
            for i2 in range(i1 + 1, EXPERTS_PER_GROUP):
                pair = v[i1] + v[i2]
                best = pair if best is None else jnp.maximum(best, pair)
        group_scores.append(best)
    grp = jnp.zeros((1, tm), jnp.int32)
    grp_val = group_scores[0]
    for g in range(1, N_GROUPS):
        upd = group_scores[g] > grp_val
        grp = jnp.where(upd, g, grp)
        grp_val = jnp.where(upd, group_scores[g], grp_val)

    def in_group(rows, i):
        out = rows[i]
        for g in range(1, N_GROUPS):
            out = jnp.where(grp == g, rows[g * EXPERTS_PER_GROUP + i], out)
        return out

    cand_b = [in_group(b_rows, i) for i in range(EXPERTS_PER_GROUP)]
    cand_s = [in_group(s_rows, i) for i in range(EXPERTS_PER_GROUP)]
    top1 = jnp.zeros((1, tm), jnp.int32)
    v1 = cand_b[0]
    s1 = cand_s[0]
    for i in range(1, EXPERTS_PER_GROUP):
        upd = cand_b[i] > v1
        top1 = jnp.where(upd, i, top1)
        v1 = jnp.where(upd, cand_b[i], v1)
        s1 = jnp.where(upd, cand_s[i], s1)
    top2 = jnp.full((1, tm), -1, jnp.int32)
    v2 = jnp.full((1, tm), -jnp.inf, F32)
    s2 = jnp.zeros((1, tm), F32)
    for i in range(EXPERTS_PER_GROUP):
        upd = (top1 != i) & ((cand_b[i] > v2) | (top2 < 0))
        top2 = jnp.where(upd, i, top2)
        v2 = jnp.where(upd, cand_b[i], v2)
        s2 = jnp.where(upd, cand_s[i], s2)
    tot = s1 + s2
    w1 = s1 / tot
    w2 = s2 / tot
    first_lo = top1 < top2
    lo = jnp.where(first_lo, top1, top2)
    hi = jnp.where(first_lo, top2, top1)
    w_lo = jnp.where(first_lo, w1, w2)
    w_hi = jnp.where(first_lo, w2, w1)
    pair_idx = jnp.where(lo == 0, hi - 1, jnp.where(lo == 1, hi + 1, 5))
    cls = grp * PAIRS_PER_GROUP + pair_idx

    ncls = carry_ref.shape[0]
    onehot = (lax.broadcasted_iota(jnp.int32, (ncls, tm), 0) == cls).astype(F32)
    before = (lax.broadcasted_iota(jnp.int32, (tm, tm), 0)
              < lax.broadcasted_iota(jnp.int32, (tm, tm), 1)).astype(BF16)
    excl = jnp.dot(onehot.astype(BF16), before, preferred_element_type=F32)
    carry = carry_ref[...]
    rank = jnp.sum(onehot * (excl + carry[:, 0:1]), axis=0, keepdims=True)
    carry = carry + jnp.sum(onehot, axis=1, keepdims=True)
    carry_ref[...] = carry
    cnt_ref[...] = carry.astype(jnp.int32)
    info_ref[...] = jnp.concatenate([cls, rank.astype(jnp.int32),
                                     jnp.zeros((SUBLANES - 2, tm), jnp.int32)], axis=0)

    wrows = jnp.concatenate([w_lo, w_hi, jnp.zeros((LANES - 2, tm), F32)], axis=0)
    hext_ref[:, d_model:] = wrows.T


def _outproj_router(ys, ws, x, mod, g, router_wt, router_bias, seq):
    t, d = x.shape
    tm = ROUTER_TILE
    per_seq = seq // tm
    n_y = len(ys)
    ncls = 32
    specs = [pl.BlockSpec((tm, y.shape[1]), lambda i: (i, 0)) for y in ys]
    specs += [pl.BlockSpec(w.shape, lambda i: (0, 0)) for w in ws]
    specs += [pl.BlockSpec((tm, d), lambda i: (i, 0)),
              pl.BlockSpec((1, 6, d), lambda i: (i // per_seq, 0, 0)),
              pl.BlockSpec((1, d), lambda i: (0, 0)),
              pl.BlockSpec(router_wt.shape, lambda i: (0, 0)),
              pl.BlockSpec(router_bias.shape, lambda i: (0, 0))]
    return pl.pallas_call(
        functools.partial(_outproj_router_body, n_y=n_y, d_model=d),
        grid=(t // tm,),
        in_specs=specs,
        out_specs=[pl.BlockSpec((tm, d), lambda i: (i, 0)),
                   pl.BlockSpec((tm, d + EXT_COLS), lambda i: (i, 0)),
                   pl.BlockSpec((SUBLANES, tm), lambda i: (0, i)),
                   pl.BlockSpec((ncls, LANES), lambda i: (0, 0))],
        out_shape=[jax.ShapeDtypeStruct((t, d), F32),
                   jax.ShapeDtypeStruct((t, d + EXT_COLS), F32),
                   jax.ShapeDtypeStruct((SUBLANES, t), jnp.int32),
                   jax.ShapeDtypeStruct((ncls, LANES), jnp.int32)],
        scratch_shapes=[pltpu.VMEM((ncls, LANES), F32)],
        compiler_params=_params(1, 48),
        name="outproj_router",
    )(*ys, *ws, x, mod, g, router_wt, router_bias)


def _permute_body(idx_ref, src_ref, *rest, scatter, chunk, window):
    dst_ref, sem = rest[-2], rest[-1]
    base = pl.program_id(0) * chunk

    def row_copy(t):
        j = idx_ref[base + t]
        if scatter:
            return pltpu.make_async_copy(src_ref.at[pl.ds(base + t, 1)], dst_ref.at[pl.ds(j, 1)], sem)
        return pltpu.make_async_copy(src_ref.at[pl.ds(j, 1)], dst_ref.at[pl.ds(base + t, 1)], sem)

    def issue(t, _):
        row_copy(t).start()

        @pl.when(t >= window)
        def _():
            row_copy(t - window).wait()
        return 0

    lax.fori_loop(0, chunk, issue, 0)

    def drain(t, _):
        row_copy(t).wait()
        return 0

    lax.fori_loop(chunk - window, chunk, drain, 0)


def _permute_rows(src, idx, n_out, scatter):
    n = idx.shape[0]
    cols = src.shape[1]
    chunk = min(PERMUTE_CHUNK, n)
    any_spec = pl.BlockSpec(memory_space=pl.ANY)
    args = [idx, src]
    in_specs = [any_spec]
    aliases = {}
    if scatter:
        args.append(jnp.zeros((n_out, cols), src.dtype))
        in_specs.append(any_spec)
        aliases = {2: 0}
    return pl.pallas_call(
        functools.partial(_permute_body, scatter=scatter, chunk=chunk, window=DMA_WINDOW),
        grid_spec=pltpu.PrefetchScalarGridSpec(
            num_scalar_prefetch=1,
            grid=(n // chunk,),
            in_specs=in_specs,
            out_specs=any_spec,
            scratch_shapes=[pltpu.SemaphoreType.DMA(())],
        ),
        out_shape=jax.ShapeDtypeStruct((n_out, cols), src.dtype),
        input_output_aliases=aliases,
        compiler_params=_params(1, 16),
        name="scatter_rows" if scatter else "gather_rows",
    )(*args)


def _expert_body(elo_ref, ehi_ref, used_ref, xs_ref, wg1, wu1, wd1, wg2, wu2, wd2, o_ref, *, d_model):
    i = pl.program_id(0)

    @pl.when(used_ref[i] != 0)
    def _():
        xe = xs_ref[...]
        x = xe[:, :d_model].astype(BF16)
        w_lo = xe[:, d_model:d_model + 1]
        w_hi = xe[:, d_model + 1:d_model + 2]

        def ffn(wg, wu, wd):
            gate = jnp.dot(x, wg[0], preferred_element_type=F32)
            up = jnp.dot(x, wu[0], preferred_element_type=F32)
            mid = (jax.nn.silu(gate) * up).astype(BF16)
            return jnp.dot(mid, wd[0], preferred_element_type=F32)

        o_ref[...] = w_lo * ffn(wg1, wu1, wd1) + w_hi * ffn(wg2, wu2, wd2)

    @pl.when(used_ref[i] == 0)
    def _():
        o_ref[...] = jnp.zeros_like(o_ref)


def _experts(xs, blk_lo, blk_hi, blk_used, w_gate, w_up, w_down, d_model):
    cap = xs.shape[0]
    bm = EXPERT_ROWS
    ff = w_gate.shape[2]

    def wspec(shape, which):
        if which == 0:
            return pl.BlockSpec((1,) + shape, lambda i, lo, hi, used: (lo[i], 0, 0))
        return pl.BlockSpec((1,) + shape, lambda i, lo, hi, used: (hi[i], 0, 0))

    return pl.pallas_call(
        functools.partial(_expert_body, d_model=d_model),
        grid_spec=pltpu.PrefetchScalarGridSpec(
            num_scalar_prefetch=3,
            grid=(cap // bm,),
            in_specs=[pl.BlockSpec((bm, xs.shape[1]), lambda i, lo, hi, used: (i, 0)),
                      wspec((d_model, ff), 0), wspec((d_model, ff), 0), wspec((ff, d_model), 0),
                      wspec((d_model, ff), 1), wspec((d_model, ff), 1), wspec((ff, d_model), 1)],
            out_specs=pl.BlockSpec((bm, d_model), lambda i, lo, hi, used: (i, 0)),
        ),
        out_shape=jax.ShapeDtypeStruct((cap, d_model), F32),
        compiler_params=_params(1, 48),
        name="expert_ffn",
    )(blk_lo, blk_hi, blk_used, xs, w_gate, w_up, w_down, w_gate, w_up, w_down)


def _class_tables():
    lo, hi = [], []
    for g in range(N_GROUPS):
        for a in range(EXPERTS_PER_GROUP):
            for b in range(a + 1, EXPERTS_PER_GROUP):
                lo.append(g * EXPERTS_PER_GROUP + a)
                hi.append(g * EXPERTS_PER_GROUP + b)
    return jnp.asarray(lo, jnp.int32), jnp.asarray(hi, jnp.int32)


def _dispatch_plan(info, counts, n_blocks):
    cls, rank = info[0], info[1]
    cnt = counts[:N_CLASSES, 0]
    blocks = (cnt + EXPERT_ROWS - 1) // EXPERT_ROWS
    ends = jnp.cumsum(blocks)
    starts = ends - blocks
    dest = starts[cls] * EXPERT_ROWS + rank
    blk = jnp.arange(n_blocks, dtype=jnp.int32)
    used = (blk < ends[-1]).astype(jnp.int32)
    blk_cls = jnp.minimum(jnp.sum((blk[:, None] >= ends[None, :]).astype(jnp.int32), axis=1), N_CLASSES - 1)
    last_cls = blk_cls[jnp.maximum(ends[-1] - 1, 0)]
    blk_cls = jnp.where(used == 1, blk_cls, last_cls).astype(jnp.int32)
    cls_lo, cls_hi = _class_tables()
    return dest.astype(jnp.int32), cls_lo[blk_cls], cls_hi[blk_cls], used


def _final_body(x_ref, delta_ref, mod_ref, g_ref, o_ref):
    x = x_ref[...] + mod_ref[0][5:6] * delta_ref[...]
    ms = jnp.mean(x * x, axis=-1, keepdims=True)
    o_ref[...] = x * lax.rsqrt(ms + EPS) * g_ref[...]


def _final_norm(x, delta, mod, g, seq):
    t, d = x.shape
    tm = TOKEN_TILE
    per_seq = seq // tm
    row = pl.BlockSpec((tm, d), lambda i: (i, 0))
    return pl.pallas_call(
        _final_body,
        grid=(t // tm,),
        in_specs=[row, row, pl.BlockSpec((1, 6, d), lambda i: (i // per_seq, 0, 0)),
                  pl.BlockSpec((1, d), lambda i: (0, 0))],
        out_specs=row,
        out_shape=jax.ShapeDtypeStruct((t, d), F32),
        compiler_params=_params(1, 32),
        name="final_norm",
    )(x, delta, mod, g)


def kernel(x, c, positions, mod_w, mod_b, norm_mix_g, norm_ffn_g, final_norm_g, w_in_ab, conv_a_w, conv_b_w, conv_b_b, lru_wa, lru_ba, lru_wx, lru_bx, lru_lambda, w_out_ab, w_qkv, w_o, router_w, router_bias, expert_w_gate, expert_w_up, expert_w_down):
    bsz, seq, d = x.shape
    depth = mod_w.shape[0]
    t = bsz * seq
    conv_width = conv_a_w.shape[2]
    n_blocks = t // EXPERT_ROWS + N_CLASSES
    cap = n_blocks * EXPERT_ROWS

    mods = _modulation(c, mod_w, mod_b).reshape(depth, bsz, 6, d)
    xt = x.reshape(t, d)
    pos = positions.reshape(t, 1)
    inv_freq = ROPE_THETA ** (-jnp.arange(0, ROT_DIM, 2, dtype=F32) / ROT_DIM)
    inv_freq_lanes = jnp.tile(inv_freq, LANES // inv_freq.shape[0]).reshape(1, LANES)
    router_wt = router_w.T
    router_b = router_bias.reshape(-1, 1)

    delta = None
    for l in range(depth):
        j = l // 2
        mod_prev = mods[l - 1] if l > 0 else None
        g_mix = norm_mix_g[l].reshape(1, d)
        if l % 2 == 0:
            xt, z = _inproj(xt, delta, mod_prev, mods[l], g_mix, w_in_ab[j].astype(BF16), seq)
            gate_w = _gate_weights(lru_wa[j], lru_wx[j])
            ya, yb = _mixer(z, seq, conv_a_w[j], conv_b_w[j], conv_b_b[j], gate_w,
                            lru_ba[j], lru_bx[j], lru_lambda[j])
            w_out = w_out_ab[j].astype(BF16)
            ys, ws = [ya, yb], [w_out[:conv_width], w_out[conv_width:]]
        else:
            xt, qkv = _inproj(xt, delta, mod_prev, mods[l], g_mix, w_qkv[j].astype(BF16), seq,
                              positions=pos, inv_freq=inv_freq_lanes)
            o = _attention(qkv, seq, d)
            ys, ws = [o], [w_o[j].astype(BF16)]
        xt, hext, info, counts = _outproj_router(ys, ws, xt, mods[l], norm_ffn_g[l].reshape(1, d),
                                                 router_wt, router_b, seq)
        dest, blk_lo, blk_hi, blk_used = _dispatch_plan(info, counts, n_blocks)
        xs = _permute_rows(hext, dest, cap, scatter=True)
        ysorted = _experts(xs, blk_lo, blk_hi, blk_used, expert_w_gate[l].astype(BF16),
                           expert_w_up[l].astype(BF16), expert_w_down[l].astype(BF16), d)
        delta = _permute_rows(ysorted, dest, t, scatter=False)
    out = _final_norm(xt, delta, mods[depth - 1], final_norm_g.reshape(1, d), seq)
    return out.reshape(bsz, seq, d)
```
